```python
import jax, jax.numpy as jnp
from jax import lax
import numpy as np

D_MODEL = 1024
BATCH = 16
SEQ = 256
DEPTH = 1
DEC_BATCH = 2
DEC_SEQ = 1024
PAST_LEN = 256

GRID_W = 64
D_CONV = D_MODEL
CONV_W = 3
DN_HEADS = 8
DN_HEAD_DIM = 128
D_DN = DN_HEADS * DN_HEAD_DIM
DN_CONV_W = 3
CHUNK = 64
D_FF = 2816
N_ADA = 9
N_IN = 3 * D_CONV + 4 * D_DN + 2 * D_MODEL + 4 * DN_HEADS
EPS = 1e-6
POS_BASE = 10000.0

kernel_name = 'hybrid_conv_deltanet_flow_step'


def _rmsnorm(x, w):
    xf = x.astype(jnp.float32)
    y = xf * lax.rsqrt(jnp.mean(xf * xf, axis=-1, keepdims=True) + EPS)
    return (y * w.astype(jnp.float32)).astype(x.dtype)


def _modulate(x, shift, scale):
    return x * (1 + scale) + shift


def _swiglu(x, wg, wu, wd):
    return (jax.nn.silu(x @ wg) * (x @ wu)) @ wd


def _l2norm(x):
    return x * lax.rsqrt(jnp.sum(x * x, axis=-1, keepdims=True) + EPS)


def _conv_rows(x, w, n_rows):
    b, t, ch = x.shape
    width = w.shape[0]
    half = width // 2
    row_len = t // n_rows
    xp = jnp.pad(x.reshape(b, n_rows, row_len, ch), ((0, 0), (0, 0), (half, half), (0, 0)))
    y = xp[:, :, 0:row_len] * w[0]
    for i in range(1, width):
        y = y + xp[:, :, i:i + row_len] * w[i]
    return y.reshape(b, t, ch)


def _grid_pos_embed(n_rows):
    t = jnp.arange(n_rows * GRID_W)
    r = (t // GRID_W).astype(jnp.float32)
    col = (t % GRID_W).astype(jnp.float32)
    quarter = D_MODEL // 4
    omega = 1.0 / (POS_BASE ** (jnp.arange(quarter, dtype=jnp.float32) / quarter))
    ar = r[:, None] * omega
    ac = col[:, None] * omega
    return jnp.concatenate([jnp.sin(ar), jnp.cos(ar), jnp.sin(ac), jnp.cos(ac)], axis=-1)


def _gated_delta_chunked(q, k, v, g, beta, s0):
    b, h, t, dk = q.shape
    dv = v.shape[-1]
    n = t // CHUNK
    q = q.reshape(b, h, n, CHUNK, dk)
    k = k.reshape(b, h, n, CHUNK, dk)
    v = v.reshape(b, h, n, CHUNK, dv)
    g = g.reshape(b, h, n, CHUNK)
    beta = beta.reshape(b, h, n, CHUNK)
    gc = jnp.cumsum(g, axis=-1)
    incl = jnp.tril(jnp.ones((CHUNK, CHUNK), dtype=bool))
    strict = jnp.tril(jnp.ones((CHUNK, CHUNK), dtype=bool), -1)
    decay = jnp.exp(jnp.where(incl, gc[..., :, None] - gc[..., None, :], -jnp.inf))
    kb = k * beta[..., None]
    lower = jnp.where(strict, jnp.einsum('bhncd,bhnsd->bhncs', kb, k) * decay, 0.0)
    tmat = lower + jnp.eye(CHUNK, dtype=jnp.float32)
    rhs = jnp.concatenate([v * beta[..., None], kb * jnp.exp(gc)[..., None]], axis=-1)
    sol = lax.linalg.triangular_solve(tmat, rhs, left_side=True, lower=True, unit_diagonal=True)
    u = sol[..., :dv]
    w = sol[..., dv:]
    a_intra = jnp.where(incl, jnp.einsum('bhncd,bhnsd->bhncs', q, k) * decay, 0.0)
    q_dec = q * jnp.exp(gc)[..., None]
    k_dec = k * jnp.exp(gc[..., -1:] - gc)[..., None]
    g_last = jnp.exp(gc[..., -1])

    def step(s, inp):
        u_i, w_i, qd_i, kd_i, a_i, gl_i = inp
        v_new = u_i - jnp.einsum('bhck,bhkv->bhcv', w_i, s)
        o_i = jnp.einsum('bhck,bhkv->bhcv', qd_i, s) + jnp.einsum('bhcs,bhsv->bhcv', a_i, v_new)
        s = s * gl_i[..., None, None] + jnp.einsum('bhck,bhcv->bhkv', kd_i, v_new)
        return s, o_i

    xs = (jnp.moveaxis(u, 2, 0), jnp.moveaxis(w, 2, 0), jnp.moveaxis(q_dec, 2, 0),
          jnp.moveaxis(k_dec, 2, 0), jnp.moveaxis(a_intra, 2, 0), jnp.moveaxis(g_last, 2, 0))
    s_final, o = lax.scan(step, s0, xs)
    o = jnp.moveaxis(o, 0, 2).reshape(b, h, t, dv)
    return o, s_final


def _mixer(u, s0_f, s0_b, n_rows, p):
    b, t, _ = u.shape
    proj = u @ p['w_in']
    bg = proj[..., 0:D_CONV]
    cg = proj[..., D_CONV:2 * D_CONV]
    xa = proj[..., 2 * D_CONV:3 * D_CONV]
    o1 = 3 * D_CONV
    qkv = proj[..., o1:o1 + 3 * D_DN]
    z = proj[..., o1 + 3 * D_DN:o1 + 4 * D_DN]
    o2 = o1 + 4 * D_DN
    ga = proj[..., o2:o2 + D_MODEL]
    gb = proj[..., o2 + D_MODEL:o2 + 2 * D_MODEL]
    o3 = o2 + 2 * D_MODEL
    b_raw = proj[..., o3:o3 + 2 * DN_HEADS].reshape(b, t, 2, DN_HEADS).astype(jnp.float32)
    a_raw = proj[..., o3 + 2 * DN_HEADS:o3 + 4 * DN_HEADS].reshape(b, t, 2, DN_HEADS).astype(jnp.float32)

    y_a = (bg * _conv_rows(cg * xa, p['conv_w'], n_rows)) @ p['conv_out_w']

    qkv = jax.nn.silu(_conv_rows(qkv, p['dn_conv_w'], n_rows)).astype(jnp.float32)
    qkv = qkv.reshape(b, t, 3, DN_HEADS, DN_HEAD_DIM).transpose(2, 0, 3, 1, 4)
    q = _l2norm(qkv[0]) * (DN_HEAD_DIM ** -0.5)
    k = _l2norm(qkv[1])
    v = qkv[2]
    beta = jax.nn.sigmoid(b_raw).transpose(2, 0, 3, 1)
    g = (-jnp.exp(p['dn_a_log'].astype(jnp.float32))
         * jax.nn.softplus(a_raw + p['dn_dt_bias'].astype(jnp.float32))).transpose(2, 0, 3, 1)
    o_f, s_f = _gated_delta_chunked(q, k, v, g[0], beta[0], s0_f.astype(jnp.float32))
    o_b, s_b = _gated_delta_chunked(jnp.flip(q, 2), jnp.flip(k, 2), jnp.flip(v, 2),
                                    jnp.flip(g[1], -1), jnp.flip(beta[1], -1), s0_b.astype(jnp.float32))
    o = (o_f + jnp.flip(o_b, 2)).transpose(0, 2, 1, 3)
    o = (o * lax.rsqrt(jnp.mean(o * o, axis=-1, keepdims=True) + EPS)
         * p['dn_norm_w'].astype(jnp.float32)
         * jax.nn.silu(z.astype(jnp.float32).reshape(b, t, DN_HEADS, DN_HEAD_DIM)))
    y_b = o.reshape(b, t, D_DN).astype(u.dtype) @ p['dn_out_w']

    mix = (jax.nn.sigmoid(ga) * y_a + jax.nn.sigmoid(gb) * y_b) @ p['w_o']
    return mix, s_f, s_b


def _layer(x, cond, s0_f, s0_b, n_rows, p):
    mod = (jax.nn.silu(cond) @ p['ada_w'] + p['ada_b']).reshape(cond.shape[0], 1, N_ADA, D_MODEL)
    sh1, sc1, g1 = mod[:, :, 0], mod[:, :, 1], mod[:, :, 2]
    sh2, sc2, g2 = mod[:, :, 3], mod[:, :, 4], mod[:, :, 5]
    sh3, sc3, g3 = mod[:, :, 6], mod[:, :, 7], mod[:, :, 8]
    x = x + 0.5 * g1 * _swiglu(_modulate(_rmsnorm(x, p['norm_ffn1']), sh1, sc1),
                               p['ffn1_w_gate'], p['ffn1_w_up'], p['ffn1_w_down'])
    mix, s_f, s_b = _mixer(_modulate(_rmsnorm(x, p['norm_mix']), sh2, sc2), s0_f, s0_b, n_rows, p)
    x = x + g2 * mix
    x = x + 0.5 * g3 * _swiglu(_modulate(_rmsnorm(x, p['norm_ffn2']), sh3, sc3),
                               p['ffn2_w_gate'], p['ffn2_w_up'], p['ffn2_w_down'])
    return x, s_f, s_b


def setup_inputs(seed: int = 0) -> dict:
    key = jax.random.key(seed)
    ks = jax.random.split(key, 32)
    f32 = jnp.float32
    L = DEPTH

    def nrm(k, shape, scale):
        return jax.random.normal(k, shape, f32) * scale

    dt = jnp.exp(jax.random.uniform(ks[18], (L, 2, DN_HEADS), f32, np.log(1e-3), np.log(1e-1)))
    return {
        'x_prompt': nrm(ks[0], (BATCH, SEQ, D_MODEL), 1.0),
        'x_sample': nrm(ks[1], (DEC_BATCH, DEC_SEQ, D_MODEL), 1.0),
        'state_dn_fwd': nrm(ks[2], (DEC_BATCH, L, DN_HEADS, DN_HEAD_DIM, DN_HEAD_DIM), 0.05),
        'state_dn_bwd': nrm(ks[3], (DEC_BATCH, L, DN_HEADS, DN_HEAD_DIM, DN_HEAD_DIM), 0.05),
        'c': nrm(ks[4], (DEC_BATCH, D_MODEL), 1.0),
        'c_ctx': nrm(ks[5], (D_MODEL,), 1.0),
        'ada_w': nrm(ks[6], (L, D_MODEL, N_ADA * D_MODEL), 0.5 * D_MODEL ** -0.5),
        'ada_b': nrm(ks[7], (L, N_ADA * D_MODEL), 0.02),
        'norm_ffn1': 1.0 + nrm(ks[8], (L, D_MODEL), 0.02),
        'ffn1_w_gate': nrm(ks[9], (L, D_MODEL, D_FF), D_MODEL ** -0.5),
        'ffn1_w_up': nrm(ks[10], (L, D_MODEL, D_FF), D_MODEL ** -0.5),
        'ffn1_w_down': nrm(ks[11], (L, D_FF, D_MODEL), D_FF ** -0.5),
        'norm_mix': 1.0 + nrm(ks[12], (L, D_MODEL), 0.02),
        'w_in': nrm(ks[13], (L, D_MODEL, N_IN), D_MODEL ** -0.5),
        'conv_w': nrm(ks[14], (L, CONV_W, D_CONV), CONV_W ** -0.5),
        'conv_out_w': nrm(ks[15], (L, D_CONV, D_MODEL), D_CONV ** -0.5),
        'dn_conv_w': nrm(ks[16], (L, DN_CONV_W, 3 * D_DN), DN_CONV_W ** -0.5),
        'dn_a_log': jnp.log(jax.random.uniform(ks[17], (L, 2, DN_HEADS), f32, 1.0, 16.0)),
        'dn_dt_bias': dt + jnp.log(-jnp.expm1(-dt)),
        'dn_norm_w': 1.0 + nrm(ks[19], (L, DN_HEAD_DIM), 0.02),
        'dn_out_w': nrm(ks[20], (L, D_DN, D_MODEL), D_DN ** -0.5),
        'w_o': nrm(ks[21], (L, D_MODEL, D_MODEL), D_MODEL ** -0.5),
        'norm_ffn2': 1.0 + nrm(ks[22], (L, D_MODEL), 0.02),
        'ffn2_w_gate': nrm(ks[23], (L, D_MODEL, D_FF), D_MODEL ** -0.5),
        'ffn2_w_up': nrm(ks[24], (L, D_MODEL, D_FF), D_MODEL ** -0.5),
        'ffn2_w_down': nrm(ks[25], (L, D_FF, D_MODEL), D_FF ** -0.5),
        'norm_f': 1.0 + nrm(ks[26], (D_MODEL,), 0.02),
    }


def reference(x_prompt, x_sample, state_dn_fwd, state_dn_bwd, c, c_ctx, ada_w, ada_b,
              norm_ffn1, ffn1_w_gate, ffn1_w_up, ffn1_w_down, norm_mix, w_in, conv_w,
              conv_out_w, dn_conv_w, dn_a_log, dn_dt_bias, dn_norm_w, dn_out_w, w_o,
              norm_ffn2, ffn2_w_gate, ffn2_w_up, ffn2_w_down, norm_f):
    def layer_params(l):
        return {'ada_w': ada_w[l], 'ada_b': ada_b[l], 'norm_ffn1': norm_ffn1[l],
                'ffn1_w_gate': ffn1_w_gate[l], 'ffn1_w_up': ffn1_w_up[l], 'ffn1_w_down': ffn1_w_down[l],
                'norm_mix': norm_mix[l], 'w_in': w_in[l], 'conv_w': conv_w[l], 'conv_out_w': conv_out_w[l],
                'dn_conv_w': dn_conv_w[l], 'dn_a_log': dn_a_log[l], 'dn_dt_bias': dn_dt_bias[l],
                'dn_norm_w': dn_norm_w[l], 'dn_out_w': dn_out_w[l], 'w_o': w_o[l],
                'norm_ffn2': norm_ffn2[l], 'ffn2_w_gate': ffn2_w_gate[l], 'ffn2_w_up': ffn2_w_up[l],
                'ffn2_w_down': ffn2_w_down[l]}

    h = x_prompt
    zero_state = jnp.zeros((x_prompt.shape[0], DN_HEADS, DN_HEAD_DIM, DN_HEAD_DIM), jnp.float32)
    states_f = []
    states_b = []
    for l in range(DEPTH):
        h, s_f, s_b = _layer(h, c_ctx[None, :], zero_state, zero_state, 1, layer_params(l))
        states_f.append(s_f)
        states_b.append(s_b)
    y_prompt = _rmsnorm(h, norm_f)
    new_state_dn_fwd = jnp.stack(states_f, axis=1).astype(x_prompt.dtype)
    new_state_dn_bwd = jnp.stack(states_b, axis=1).astype(x_prompt.dtype)

    rows = x_sample.shape[1] // GRID_W
    zt = x_sample + _grid_pos_embed(rows).astype(x_sample.dtype)[None]
    for l in range(DEPTH):
        zt, _, _ = _layer(zt, c, state_dn_fwd[:, l], state_dn_bwd[:, l], rows, layer_params(l))
    y_sample = _rmsnorm(zt, norm_f)
    return (y_prompt, y_sample, new_state_dn_fwd, new_state_dn_bwd)
```

```python
import functools

import jax
import jax.numpy as jnp
from jax import lax
from jax.experimental import pallas as pl
from jax.experimental.pallas import tpu as pltpu

D_MODEL = 1024
D_FF = 2816
DN_HEADS = 8
DN_HEAD_DIM = 128
N_ADA = 9
GRID_W = 64
EPS = 1e-6
POS_BASE = 10000.0

CHUNK = 256
ROW_TILE = 512
FF_CHUNK = 256
GATE_LANES = 128
VMEM_LIMIT = 56 * 1024 * 1024

_F32 = jnp.float32
_BF16 = jnp.bfloat16


def _dot(a, b):
    return jnp.dot(a, b, preferred_element_type=_F32)


def _dot_nt(a, b):
    return lax.dot_general(a, b, (((1,), (1,)), ((), ())), preferred_element_type=_F32)


def _dot_tn(a, b):
    return lax.dot_general(a, b, (((0,), (0,)), ((), ())), preferred_element_type=_F32)


def _sigmoid(x):
    return 1.0 / (1.0 + jnp.exp(-x))


def _silu(x):
    return x * _sigmoid(x)


def _rms(x):
    return x * lax.rsqrt(jnp.mean(x * x, axis=-1, keepdims=True) + EPS)


def _resident(shape):
    nd = len(shape)
    return pl.BlockSpec(shape, lambda *_: (0,) * nd, pipeline_mode=pl.Buffered(1))


def _params():
    return pltpu.CompilerParams(dimension_semantics=("arbitrary",), vmem_limit_bytes=VMEM_LIMIT)


def _ada_kernel(ct_ref, w_ref, b_ref, o_ref):
    ct = ct_ref[...]
    s = _silu(ct)
    w = w_ref[...]
    o_ref[...] = jnp.zeros_like(o_ref)
    for r in range(3):
        o_ref[r:r + 1, :] = jnp.sum(s[:, r:r + 1] * w, axis=0, keepdims=True) + b_ref[...]


def _ada(cond_t, ada_w, ada_b):
    n = ada_w.shape[1]
    tn = 1024
    return pl.pallas_call(
        _ada_kernel,
        grid=(n // tn,),
        in_specs=[pl.BlockSpec((D_MODEL, 8), lambda j: (0, 0)),
                  pl.BlockSpec((D_MODEL, tn), lambda j: (0, j)),
                  pl.BlockSpec((1, tn), lambda j: (0, j))],
        out_specs=pl.BlockSpec((8, tn), lambda j: (0, j)),
        out_shape=jax.ShapeDtypeStruct((8, n), _F32),
        compiler_params=_params(),
        name="ada",
    )(cond_t, ada_w, ada_b)


def _ffn(x, nw, sh, sc, gate, wg_ref, wu_ref, wd_ref):
    u = (_rms(x) * nw * (1.0 + sc) + sh).astype(_BF16)
    acc = None
    for c in range(D_FF // FF_CHUNK):
        sl = slice(c * FF_CHUNK, (c + 1) * FF_CHUNK)
        a = _dot(u, wg_ref[:, sl])
        b = _dot(u, wu_ref[:, sl])
        h = (_silu(a) * b).astype(_BF16)
        d = _dot(h, wd_ref[sl, :])
        acc = d if acc is None else acc + d
    return x + (0.5 * gate) * acc


def _ffn1_kernel(x_ref, mod_ref, nw_ref, wg_ref, wu_ref, wd_ref, o_ref):
    o_ref[...] = _ffn(x_ref[...], nw_ref[...], mod_ref[0:1, :], mod_ref[1:2, :], mod_ref[2:3, :],
                      wg_ref, wu_ref, wd_ref)


def _group_of_tile(n_ctx_tiles, tiles_per_latent):
    def f(i):
        return jnp.where(i < n_ctx_tiles, 0, 1 + (i - n_ctx_tiles) // tiles_per_latent)
    return f


def _row_spec(width):
    return pl.BlockSpec((ROW_TILE, width), lambda i: (i, 0))


def _mod_spec(group):
    return pl.BlockSpec((None, N_ADA, D_MODEL), lambda i: (group(i), 0, 0))


def _ffn1(x, mod, group, nw, wg, wu, wd):
    t = x.shape[0]
    return pl.pallas_call(
        _ffn1_kernel,
        grid=(t // ROW_TILE,),
        in_specs=[_row_spec(D_MODEL), _mod_spec(group), _resident((1, D_MODEL)),
                  _resident(wg.shape), _resident(wu.shape), _resident(wd.shape)],
        out_specs=_row_spec(D_MODEL),
        out_shape=jax.ShapeDtypeStruct((t, D_MODEL), _F32),
        compiler_params=_params(),
        name="ffn1",
    )(x, mod, nw, wg, wu, wd)


def _conv3(x, w_ref, lanes, pos, last):
    m = x.shape[0]
    prev = jnp.where(pos == 0, 0.0, pltpu.roll(x, 1, axis=0))
    nxt = jnp.where(pos == last, 0.0, pltpu.roll(x, m - 1, axis=0))
    return prev * w_ref[0:1, lanes] + x * w_ref[1:2, lanes] + nxt * w_ref[2:3, lanes]


def _split3(x):
    hi = x.astype(_BF16)
    r = x - hi.astype(_F32)
    mid = r.astype(_BF16)
    lo = (r - mid.astype(_F32)).astype(_BF16)
    return hi, mid, lo


def _mix_in_kernel(n_ctx_tiles, x_ref, mod_ref, nw_ref, win_ref, wba_ref, cw_ref, cow_ref, dcw_ref,
                   alog_ref, dtb_ref, ma_ref, q_ref, k_ref, v_ref, sz_ref, sgb_ref, g_ref, gt_ref):
    i = pl.program_id(0)
    d = D_MODEL
    uf = _rms(x_ref[...]) * nw_ref[...] * (1.0 + mod_ref[4:5, :]) + mod_ref[3:4, :]
    u = uf.astype(_BF16)

    row = lax.broadcasted_iota(jnp.int32, (ROW_TILE, 1), 0)
    last = jnp.where(i < n_ctx_tiles, CHUNK - 1, GRID_W - 1)
    pos = row & last
    full = slice(0, d)

    bg = _dot(u, win_ref[:, 0:d])
    cg = _dot(u, win_ref[:, d:2 * d])
    xa = _dot(u, win_ref[:, 2 * d:3 * d])
    a_pre = (bg * _conv3(cg * xa, cw_ref, full, pos, last)).astype(_BF16)
    ya = _dot(a_pre, cow_ref[...])
    ga = _dot(u, win_ref[:, 7 * d:8 * d])
    ma_ref[...] = (_sigmoid(ga) * ya).astype(_BF16)

    for part, ref in enumerate((q_ref, k_ref, v_ref)):
        p = _dot(u, win_ref[:, (3 + part) * d:(4 + part) * d])
        p = _silu(_conv3(p, dcw_ref, slice(part * d, (part + 1) * d), pos, last))
        if part == 2:
            ref[...] = p.astype(_BF16)
        else:
            scale = DN_HEAD_DIM ** -0.5 if part == 0 else 1.0
            for h in range(DN_HEADS):
                ph = p[:, h * DN_HEAD_DIM:(h + 1) * DN_HEAD_DIM]
                ph = ph * lax.rsqrt(jnp.sum(ph * ph, axis=-1, keepdims=True) + EPS)
                ref[:, h * DN_HEAD_DIM:(h + 1) * DN_HEAD_DIM] = (ph * scale).astype(_BF16)

    sz_ref[...] = _silu(_dot(u, win_ref[:, 6 * d:7 * d])).astype(_BF16)
    sgb_ref[...] = _sigmoid(_dot(u, win_ref[:, 8 * d:9 * d])).astype(_BF16)

    u_lo = (uf - u.astype(_F32)).astype(_BF16)
    wba = wba_ref[...]
    w_hi = wba.astype(_BF16)
    w_lo = (wba - w_hi.astype(_F32)).astype(_BF16)
    ba = _dot(u, w_hi) + _dot(u, w_lo) + _dot(u_lo, w_hi)
    beta = _sigmoid(ba)
    z = ba + dtb_ref[...]
    softplus = jnp.maximum(z, 0.0) + jnp.log(1.0 + jnp.exp(-jnp.abs(z)))
    g = -jnp.exp(alog_ref[...]) * softplus

    lane = lax.broadcasted_iota(jnp.int32, (CHUNK, GATE_LANES), 1)
    ri = lax.broadcasted_iota(jnp.int32, (CHUNK, CHUNK), 0)
    ci = lax.broadcasted_iota(jnp.int32, (CHUNK, CHUNK), 1)
    lower = jnp.where(ri >= ci, 1.0, 0.0).astype(_BF16)
    upper = jnp.where(ri <= ci, 1.0, 0.0).astype(_BF16)
    nh = DN_HEADS
    for c in range(ROW_TILE // CHUNK):
        rows = slice(c * CHUNK, (c + 1) * CHUNK)
        parts = _split3(g[rows, :])
        pre = sum(_dot(lower, t) for t in parts)
        suf = sum(_dot(upper, t) for t in parts)
        tab = jnp.where(lane < 2 * nh, beta[rows, :],
                        jnp.where(lane < 3 * nh, pre, jnp.where(lane < 4 * nh, suf, 0.0)))
        g_ref[rows, :] = tab
        gt_ref[c] = tab.T


def _mix_in(x, mod, group, n_ctx_tiles, nw, w_in, w_ba, conv_w, conv_out_w, dn_conv_w, alog_row, dtb_row):
    t = x.shape[0]
    bf = lambda w: jax.ShapeDtypeStruct((t, w), _BF16)
    return pl.pallas_call(
        functools.partial(_mix_in_kernel, n_ctx_tiles),
        grid=(t // ROW_TILE,),
        in_specs=[_row_spec(D_MODEL), _mod_spec(group), _resident((1, D_MODEL)),
                  _resident(w_in.shape), _resident(w_ba.shape), _resident(conv_w.shape),
                  _resident(conv_out_w.shape), _resident(dn_conv_w.shape),
                  _resident((1, GATE_LANES)), _resident((1, GATE_LANES))],
        out_specs=[_row_spec(D_MODEL)] * 6 + [
            _row_spec(GATE_LANES),
            pl.BlockSpec((ROW_TILE // CHUNK, GATE_LANES, CHUNK), lambda i: (i, 0, 0))],
        out_shape=[bf(D_MODEL)] * 6 + [jax.ShapeDtypeStruct((t, GATE_LANES), _F32),
                                       jax.ShapeDtypeStruct((t // CHUNK, GATE_LANES, CHUNK), _F32)],
        compiler_params=_params(),
        name="mix_in",
    )(x, mod, nw, w_in, w_ba, conv_w, conv_out_w, dn_conv_w, alog_row, dtb_row)


def _inv_unit_tri(l_mat, xor_ij):
    tinv = jnp.where(xor_ij == 0, 1.0, 0.0) - jnp.where(xor_ij == 1, l_mat, 0.0)
    m = 2
    while m < CHUNK:
        c = jnp.where((xor_ij >= m) & (xor_ij < 2 * m), l_mat, 0.0).astype(_BF16)
        t16 = tinv.astype(_BF16)
        tinv = tinv - _dot(t16, _dot(c, t16).astype(_BF16))
        m *= 2
    return tinv


def _dn_chunk(direction, head, q, k, v, gates, gate_row, state):
    nh = DN_HEADS
    lane = lax.broadcasted_iota(jnp.int32, (CHUNK, GATE_LANES), 1)
    ri = lax.broadcasted_iota(jnp.int32, (CHUNK, CHUNK), 0)
    ci = lax.broadcasted_iota(jnp.int32, (CHUNK, CHUNK), 1)
    pick = lambda col: jnp.sum(jnp.where(lane == col, gates, 0.0), axis=1, keepdims=True)
    beta = pick(direction * nh + head)
    gc = pick((2 + direction) * nh + head)

    incl = (ri >= ci) if direction == 0 else (ri <= ci)
    decay = jnp.exp(jnp.where(incl, gc - gate_row, -jnp.inf))
    kf = k.astype(_F32)
    l_mat = jnp.where(ri != ci, beta * _dot_nt(k, k) * decay, 0.0)
    a16 = (_dot_nt(q, k) * decay).astype(_BF16)
    tinv = _inv_unit_tri(l_mat, ri ^ ci).astype(_BF16)

    g_last = gc[CHUNK - 1:CHUNK, :] if direction == 0 else gc[0:1, :]
    kd = (kf * jnp.exp(g_last - gc)).astype(_BF16)
    vb = (v.astype(_F32) * beta).astype(_BF16)
    if state is None:
        v_new = _dot(tinv, vb).astype(_BF16)
        return _dot(a16, v_new), _dot_tn(kd, v_new)
    egc = jnp.exp(gc)
    rhs = jnp.concatenate([vb, (kf * (beta * egc)).astype(_BF16)], axis=1)
    uw = _dot(tinv, rhs)
    s16 = state.astype(_BF16)
    v_new = (uw[:, :DN_HEAD_DIM] - _dot(uw[:, DN_HEAD_DIM:].astype(_BF16), s16)).astype(_BF16)
    qd = (q.astype(_F32) * egc).astype(_BF16)
    o = _dot(qd, s16) + _dot(a16, v_new)
    return o, state * jnp.exp(g_last) + _dot_tn(kd, v_new)


def _dn_ctx_kernel(q_ref, k_ref, v_ref, g_ref, gt_ref, o_ref, sf_ref, sb_ref):
    h = pl.program_id(1)
    q, k, v, gates = q_ref[...], k_ref[...], v_ref[...], g_ref[...]
    o = None
    for direction, s_ref in ((0, sf_ref), (1, sb_ref)):
        gate_row = gt_ref[pl.ds((2 + direction) * DN_HEADS + h, 1), :]
        od, s_new = _dn_chunk(direction, h, q, k, v, gates, gate_row, None)
        s_ref[...] = s_new
        o = od if o is None else o + od
    o_ref[...] = o


def _dn_lat_kernel(n_chunks, q_ref, k_ref, v_ref, g_ref, gt_ref, s0f_ref, s0b_ref, o_in_ref, o_ref):
    del o_in_ref
    h = pl.program_id(1)
    for direction, s0_ref in ((0, s0f_ref), (1, s0b_ref)):
        def body(step, state, direction=direction):
            c = step if direction == 0 else n_chunks - 1 - step
            rows = pl.ds(pl.multiple_of(c * CHUNK, CHUNK), CHUNK)
            gate_row = gt_ref[c, pl.ds((2 + direction) * DN_HEADS + h, 1), :]
            od, s_new = _dn_chunk(direction, h, q_ref[rows, :], k_ref[rows, :], v_ref[rows, :],
                                  g_ref[rows, :], gate_row, state)
            if direction == 0:
                o_ref[rows, :] = od
            else:
                o_ref[rows, :] += od
            return s_new
        lax.fori_loop(0, n_chunks, body, s0_ref[...])


def _deltanet(q, k, v, gates, gates_t, s0f, s0b, n_ctx_seq, ctx_len, n_lat_seq, lat_len):
    t = q.shape[0]
    dh = DN_HEAD_DIM
    params = pltpu.CompilerParams(dimension_semantics=("arbitrary", "arbitrary"), vmem_limit_bytes=VMEM_LIMIT)
    head_spec = lambda rows, off: pl.BlockSpec((rows, dh), lambda b, h: (b + off, h))
    state_spec = pl.BlockSpec((None, None, dh, dh), lambda b, h: (b, h, 0, 0))
    state_shape = jax.ShapeDtypeStruct((n_ctx_seq, DN_HEADS, dh, dh), _F32)

    o, sf, sb = pl.pallas_call(
        _dn_ctx_kernel,
        grid=(n_ctx_seq, DN_HEADS),
        in_specs=[head_spec(ctx_len, 0)] * 3 + [
            pl.BlockSpec((ctx_len, GATE_LANES), lambda b, h: (b, 0)),
            pl.BlockSpec((None, GATE_LANES, CHUNK), lambda b, h: (b, 0, 0))],
        out_specs=[head_spec(ctx_len, 0), state_spec, state_spec],
        out_shape=[jax.ShapeDtypeStruct((t, D_MODEL), _F32), state_shape, state_shape],
        compiler_params=params,
        name="deltanet_ctx",
    )(q, k, v, gates, gates_t)

    off = (n_ctx_seq * ctx_len) // lat_len
    o = pl.pallas_call(
        functools.partial(_dn_lat_kernel, lat_len // CHUNK),
        grid=(n_lat_seq, DN_HEADS),
        in_specs=[head_spec(lat_len, off)] * 3 + [
            pl.BlockSpec((lat_len, GATE_LANES), lambda b, h: (b + off, 0)),
            pl.BlockSpec((lat_len // CHUNK, GATE_LANES, CHUNK), lambda b, h: (b + off, 0, 0)),
            state_spec, state_spec,
            pl.BlockSpec(memory_space=pl.ANY)],
        out_specs=head_spec(lat_len, off),
        out_shape=jax.ShapeDtypeStruct((t, D_MODEL), _F32),
        input_output_aliases={7: 0},
        compiler_params=params,
        name="deltanet_lat",
    )(q, k, v, gates, gates_t, s0f, s0b, o)
    return o, sf, sb


def _mix_out_kernel(o_ref, sz_ref, sgb_ref, ma_ref, x_ref, mod_ref, dnw_ref, dow_ref, wo_ref,
                    nw_ref, wg_ref, wu_ref, wd_ref, nf_ref, y_ref):
    dh = DN_HEAD_DIM
    o = o_ref[...]
    heads = []
    for h in range(DN_HEADS):
        oh = o[:, h * dh:(h + 1) * dh]
        heads.append(oh * lax.rsqrt(jnp.mean(oh * oh, axis=-1, keepdims=True) + EPS) * dnw_ref[...])
    on = (jnp.concatenate(heads, axis=1) * sz_ref[...].astype(_F32)).astype(_BF16)
    yb = _dot(on, dow_ref[...])
    merged = (ma_ref[...].astype(_F32) + sgb_ref[...].astype(_F32) * yb).astype(_BF16)
    x = x_ref[...] + mod_ref[5:6, :] * _dot(merged, wo_ref[...])
    x = _ffn(x, nw_ref[...], mod_ref[6:7, :], mod_ref[7:8, :], mod_ref[8:9, :], wg_ref, wu_ref, wd_ref)
    y_ref[...] = _rms(x) * nf_ref[...]


def _mix_out(o, sz, sgb, ma, x, mod, group, dn_norm_w, dn_out_w, w_o, nw, wg, wu, wd, norm_f):
    t = x.shape[0]
    return pl.pallas_call(
        _mix_out_kernel,
        grid=(t // ROW_TILE,),
        in_specs=[_row_spec(D_MODEL)] * 5 + [_mod_spec(group), _resident((1, DN_HEAD_DIM)),
                  _resident(dn_out_w.shape), _resident(w_o.shape), _resident((1, D_MODEL)),
                  _resident(wg.shape), _resident(wu.shape), _resident(wd.shape), _resident((1, D_MODEL))],
        out_specs=_row_spec(D_MODEL),
        out_shape=jax.ShapeDtypeStruct((t, D_MODEL), _F32),
        compiler_params=_params(),
        name="mix_out",
    )(o, sz, sgb, ma, x, mod, dn_norm_w, dn_out_w, w_o, nw, wg, wu, wd, norm_f)


def _grid_pos_embed(n_rows):
    t = jnp.arange(n_rows * GRID_W)
    r = (t // GRID_W).astype(_F32)
    col = (t % GRID_W).astype(_F32)
    quarter = D_MODEL // 4
    omega = 1.0 / (POS_BASE ** (jnp.arange(quarter, dtype=_F32) / quarter))
    ar = r[:, None] * omega
    ac = col[:, None] * omega
    return jnp.concatenate([jnp.sin(ar), jnp.cos(ar), jnp.sin(ac), jnp.cos(ac)], axis=-1)


def kernel(x_prompt, x_sample, state_dn_fwd, state_dn_bwd, c, c_ctx, ada_w, ada_b, norm_ffn1, ffn1_w_gate, ffn1_w_up, ffn1_w_down, norm_mix, w_in, conv_w, conv_out_w, dn_conv_w, dn_a_log, dn_dt_bias, dn_norm_w, dn_out_w, w_o, norm_ffn2, ffn2_w_gate, ffn2_w_up, ffn2_w_down, norm_f):
    n_ctx_seq, ctx_len, d = x_prompt.shape
    n_lat_seq, lat_len, _ = x_sample.shape
    depth = ada_w.shape[0]
    assert depth == 1 and d == D_MODEL
    assert ctx_len == CHUNK and lat_len % CHUNK == 0 and lat_len % ROW_TILE == 0
    assert (n_ctx_seq * ctx_len) % lat_len == 0 and CHUNK % GRID_W == 0 and n_lat_seq == 2
    n_ctx_tiles = (n_ctx_seq * ctx_len) // ROW_TILE
    group = _group_of_tile(n_ctx_tiles, lat_len // ROW_TILE)
    bf = lambda w: w.astype(_BF16)
    row = lambda w: w.reshape(1, -1)

    pos = _grid_pos_embed(lat_len // GRID_W).astype(x_sample.dtype)
    x = jnp.concatenate([x_prompt.reshape(-1, d), (x_sample + pos[None]).reshape(-1, d)], axis=0)

    cond_t = jnp.zeros((d, 8), _F32).at[:, 0].set(c_ctx).at[:, 1:1 + n_lat_seq].set(c.T)
    mod = _ada(cond_t, ada_w[0], row(ada_b[0])).reshape(8, N_ADA, d)

    x = _ffn1(x, mod, group, row(norm_ffn1[0]), bf(ffn1_w_gate[0]), bf(ffn1_w_up[0]), bf(ffn1_w_down[0]))

    n_main = 9 * d
    n_gate = 4 * DN_HEADS
    w_ba = jnp.zeros((d, GATE_LANES), _F32).at[:, :n_gate].set(w_in[0][:, n_main:n_main + n_gate])
    gate_row = lambda p: jnp.zeros((1, GATE_LANES), _F32).at[0, 2 * DN_HEADS:n_gate].set(p.reshape(-1))
    ma, q, k, v, sz, sgb, gates, gates_t = _mix_in(
        x, mod, group, n_ctx_tiles, row(norm_mix[0]), bf(w_in[0][:, :n_main]), w_ba, conv_w[0],
        bf(conv_out_w[0]), dn_conv_w[0], gate_row(dn_a_log[0]), gate_row(dn_dt_bias[0]))

    o, sf, sb = _deltanet(q, k, v, gates, gates_t, state_dn_fwd[:, 0], state_dn_bwd[:, 0],
                          n_ctx_seq, ctx_len, n_lat_seq, lat_len)

    y = _mix_out(o, sz, sgb, ma, x, mod, group, row(dn_norm_w[0]), bf(dn_out_w[0]), bf(w_o[0]),
                 row(norm_ffn2[0]), bf(ffn2_w_gate[0]), bf(ffn2_w_up[0]), bf(ffn2_w_down[0]), row(norm_f))

    n_ctx = n_ctx_seq * ctx_len
    y_prompt = y[:n_ctx].reshape(x_prompt.shape)
    y_sample = y[n_ctx:].reshape(x_sample.shape)
    return (y_prompt, y_sample, sf[:, None].astype(x_prompt.dtype), sb[:, None].astype(x_prompt.dtype))
```

```python
import functools
from typing import Any, NamedTuple

import jax
import jax.numpy as jnp
from jax import lax
from jax.experimental import pallas as pl
from jax.experimental.pallas import tpu as pltpu

D_MODEL = 1024
D_FF = 2816
DN_HEADS = 8
DN_HEAD_DIM = 128
N_ADA = 9
GRID_W = 64
EPS = 1e-6
POS_BASE = 10000.0

CHUNK = 256
HALF = CHUNK // 2
HEADS_PER_STEP = 2
ROW_TILE = 512
FF_CHUNK = 256
GATE_LANES = 128
VMEM_LIMIT = 56 * 1024 * 1024

_F32 = jnp.float32
_BF16 = jnp.bfloat16


def _dot(a, b):
    return jnp.dot(a, b, preferred_element_type=_F32)


def _dot_nt(a, b):
    return lax.dot_general(a, b, (((1,), (1,)), ((), ())), preferred_element_type=_F32)


def _dot_tn(a, b):
    return lax.dot_general(a, b, (((0,), (0,)), ((), ())), preferred_element_type=_F32)


def _sigmoid(x):
    return 1.0 / (1.0 + jnp.exp(-x))


def _silu(x):
    return x * _sigmoid(x)


def _rms(x):
    return x * lax.rsqrt(jnp.mean(x * x, axis=-1, keepdims=True) + EPS)


def _resident(shape):
    nd = len(shape)
    return pl.BlockSpec(shape, lambda *_: (0,) * nd, pipeline_mode=pl.Buffered(1))


def _params():
    return pltpu.CompilerParams(dimension_semantics=("arbitrary",), vmem_limit_bytes=VMEM_LIMIT)


def _ada_kernel(ct_ref, w_ref, b_ref, o_ref):
    ct = ct_ref[...]
    s = _silu(ct)
    w = w_ref[...]
    o_ref[...] = jnp.zeros_like(o_ref)
    for r in range(3):
        o_ref[r:r + 1, :] = jnp.sum(s[:, r:r + 1] * w, axis=0, keepdims=True) + b_ref[...]


def _ada(cond_t, ada_w, ada_b):
    n = ada_w.shape[1]
    tn = 1024
    return pl.pallas_call(
        _ada_kernel,
        grid=(n // tn,),
        in_specs=[pl.BlockSpec((D_MODEL, 8), lambda j: (0, 0)),
                  pl.BlockSpec((D_MODEL, tn), lambda j: (0, j)),
                  pl.BlockSpec((1, tn), lambda j: (0, j))],
        out_specs=pl.BlockSpec((8, tn), lambda j: (0, j)),
        out_shape=jax.ShapeDtypeStruct((8, n), _F32),
        compiler_params=_params(),
        name="ada",
    )(cond_t, ada_w, ada_b)


def _ffn(x, nw, sh, sc, gate, wg_ref, wu_ref, wd_ref):
    u = (_rms(x) * nw * (1.0 + sc) + sh).astype(_BF16)
    acc = None
    for c in range(D_FF // FF_CHUNK):
        sl = slice(c * FF_CHUNK, (c + 1) * FF_CHUNK)
        a = _dot(u, wg_ref[:, sl])
        b = _dot(u, wu_ref[:, sl])
        h = (_silu(a) * b).astype(_BF16)
        d = _dot(h, wd_ref[sl, :])
        acc = d if acc is None else acc + d
    return x + (0.5 * gate) * acc


def _ffn1_kernel(x_ref, mod_ref, nw_ref, wg_ref, wu_ref, wd_ref, o_ref):
    o_ref[...] = _ffn(x_ref[...], nw_ref[...], mod_ref[0:1, :], mod_ref[1:2, :], mod_ref[2:3, :],
                      wg_ref, wu_ref, wd_ref)


def _group_of_tile(n_ctx_tiles, tiles_per_latent):
    def f(i):
        return jnp.where(i < n_ctx_tiles, 0, 1 + (i - n_ctx_tiles) // tiles_per_latent)
    return f


def _row_spec(width):
    return pl.BlockSpec((ROW_TILE, width), lambda i: (i, 0))


def _mod_spec(group):
    return pl.BlockSpec((None, N_ADA, D_MODEL), lambda i: (group(i), 0, 0))


def _ffn1(x, mod, group, nw, wg, wu, wd):
    t = x.shape[0]
    return pl.pallas_call(
        _ffn1_kernel,
        grid=(t // ROW_TILE,),
        in_specs=[_row_spec(D_MODEL), _mod_spec(group), _resident((1, D_MODEL)),
                  _resident(wg.shape), _resident(wu.shape), _resident(wd.shape)],
        out_specs=_row_spec(D_MODEL),
        out_shape=jax.ShapeDtypeStruct((t, D_MODEL), _F32),
        compiler_params=_params(),
        name="ffn1",
    )(x, mod, nw, wg, wu, wd)


def _conv3(x, w_ref, lanes, pos, last):
    m = x.shape[0]
    prev = jnp.where(pos == 0, 0.0, pltpu.roll(x, 1, axis=0))
    nxt = jnp.where(pos == last, 0.0, pltpu.roll(x, m - 1, axis=0))
    return prev * w_ref[0:1, lanes] + x * w_ref[1:2, lanes] + nxt * w_ref[2:3, lanes]


def _split3(x):
    hi = x.astype(_BF16)
    r = x - hi.astype(_F32)
    mid = r.astype(_BF16)
    lo = (r - mid.astype(_F32)).astype(_BF16)
    return hi, mid, lo


def _mix_in_kernel(n_ctx_tiles, x_ref, mod_ref, nw_ref, win_ref, wba_ref, cw_ref, cow_ref, dcw_ref,
                   alog_ref, dtb_ref, ma_ref, q_ref, k_ref, v_ref, sz_ref, sgb_ref, g_ref, gt_ref):
    i = pl.program_id(0)
    d = D_MODEL
    uf = _rms(x_ref[...]) * nw_ref[...] * (1.0 + mod_ref[4:5, :]) + mod_ref[3:4, :]
    u = uf.astype(_BF16)

    row = lax.broadcasted_iota(jnp.int32, (ROW_TILE, 1), 0)
    last = jnp.where(i < n_ctx_tiles, CHUNK - 1, GRID_W - 1)
    pos = row & last
    full = slice(0, d)

    bg = _dot(u, win_ref[:, 0:d])
    cg = _dot(u, win_ref[:, d:2 * d])
    xa = _dot(u, win_ref[:, 2 * d:3 * d])
    a_pre = (bg * _conv3(cg * xa, cw_ref, full, pos, last)).astype(_BF16)
    ya = _dot(a_pre, cow_ref[...])
    ga = _dot(u, win_ref[:, 7 * d:8 * d])
    ma_ref[...] = (_sigmoid(ga) * ya).astype(_BF16)

    for part, ref in enumerate((q_ref, k_ref, v_ref)):
        p = _dot(u, win_ref[:, (3 + part) * d:(4 + part) * d])
        p = _silu(_conv3(p, dcw_ref, slice(part * d, (part + 1) * d), pos, last))
        if part == 2:
            ref[...] = p.astype(_BF16)
        else:
            scale = DN_HEAD_DIM ** -0.5 if part == 0 else 1.0
            for h in range(DN_HEADS):
                ph = p[:, h * DN_HEAD_DIM:(h + 1) * DN_HEAD_DIM]
                ph = ph * lax.rsqrt(jnp.sum(ph * ph, axis=-1, keepdims=True) + EPS)
                ref[:, h * DN_HEAD_DIM:(h + 1) * DN_HEAD_DIM] = (ph * scale).astype(_BF16)

    sz_ref[...] = _silu(_dot(u, win_ref[:, 6 * d:7 * d])).astype(_BF16)
    sgb_ref[...] = _sigmoid(_dot(u, win_ref[:, 8 * d:9 * d])).astype(_BF16)

    u_lo = (uf - u.astype(_F32)).astype(_BF16)
    wba = wba_ref[...]
    w_hi = wba.astype(_BF16)
    w_lo = (wba - w_hi.astype(_F32)).astype(_BF16)
    ba = _dot(u, w_hi) + _dot(u, w_lo) + _dot(u_lo, w_hi)
    beta = _sigmoid(ba)
    z = ba + dtb_ref[...]
    softplus = jnp.maximum(z, 0.0) + jnp.log(1.0 + jnp.exp(-jnp.abs(z)))
    g = -jnp.exp(alog_ref[...]) * softplus

    lane = lax.broadcasted_iota(jnp.int32, (CHUNK, GATE_LANES), 1)
    ri = lax.broadcasted_iota(jnp.int32, (CHUNK, CHUNK), 0)
    ci = lax.broadcasted_iota(jnp.int32, (CHUNK, CHUNK), 1)
    lower = jnp.where(ri >= ci, 1.0, 0.0).astype(_BF16)
    upper = jnp.where(ri <= ci, 1.0, 0.0).astype(_BF16)
    nh = DN_HEADS
    for c in range(ROW_TILE // CHUNK):
        rows = slice(c * CHUNK, (c + 1) * CHUNK)
        parts = _split3(g[rows, :])
        pre = sum(_dot(lower, t) for t in parts)
        suf = sum(_dot(upper, t) for t in parts)
        tab = jnp.where(lane < 2 * nh, beta[rows, :],
                        jnp.where(lane < 3 * nh, pre, jnp.where(lane < 4 * nh, suf, 0.0)))
        g_ref[rows, :] = tab
        gt_ref[c] = tab.T


def _mix_in(x, mod, group, n_ctx_tiles, nw, w_in, w_ba, conv_w, conv_out_w, dn_conv_w, alog_row, dtb_row):
    t = x.shape[0]
    bf = lambda w: jax.ShapeDtypeStruct((t, w), _BF16)
    return pl.pallas_call(
        functools.partial(_mix_in_kernel, n_ctx_tiles),
        grid=(t // ROW_TILE,),
        in_specs=[_row_spec(D_MODEL), _mod_spec(group), _resident((1, D_MODEL)),
                  _resident(w_in.shape), _resident(w_ba.shape), _resident(conv_w.shape),
                  _resident(conv_out_w.shape), _resident(dn_conv_w.shape),
                  _resident((1, GATE_LANES)), _resident((1, GATE_LANES))],
        out_specs=[_row_spec(D_MODEL)] * 6 + [
            _row_spec(GATE_LANES),
            pl.BlockSpec((ROW_TILE // CHUNK, GATE_LANES, CHUNK), lambda i: (i, 0, 0))],
        out_shape=[bf(D_MODEL)] * 6 + [jax.ShapeDtypeStruct((t, GATE_LANES), _F32),
                                       jax.ShapeDtypeStruct((t // CHUNK, GATE_LANES, CHUNK), _F32)],
        compiler_params=_params(),
        name="mix_in",
    )(x, mod, nw, w_in, w_ba, conv_w, conv_out_w, dn_conv_w, alog_row, dtb_row)


class _Chain(NamedTuple):
    direction: int
    head: Any
    q: Any
    k: Any
    v: Any
    kk: Any
    qk: Any
    gates: Any
    gate_row: Any
    state: Any


def _inv_unit_tri(l_mats, directions):
    hs = HALF
    ri = lax.broadcasted_iota(jnp.int32, (hs, hs), 0)
    ci = lax.broadcasted_iota(jnp.int32, (hs, hs), 1)
    x = ri ^ ci
    diag = [d for l in l_mats for d in (l[:hs, :hs], l[hs:, hs:])]
    eye = jnp.where(x == 0, 1.0, 0.0)
    tinv = [eye - jnp.where(x == 1, d, 0.0) for d in diag]
    m = 2
    while m < hs:
        level = (x >= m) & (x < 2 * m)
        c16 = [jnp.where(level, d, 0.0).astype(_BF16) for d in diag]
        t16 = [t.astype(_BF16) for t in tinv]
        mid = [_dot(c, t).astype(_BF16) for c, t in zip(c16, t16)]
        tinv = [t - _dot(t6, md) for t, t6, md in zip(tinv, t16, mid)]
        m *= 2
    t16 = [t.astype(_BF16) for t in tinv]
    pairs = [(t16[2 * n], t16[2 * n + 1]) for n in range(len(l_mats))]
    off = [(l[hs:, :hs] if d == 0 else l[:hs, hs:]).astype(_BF16) for l, d in zip(l_mats, directions)]
    mid = [_dot(c, p[d]).astype(_BF16) for c, p, d in zip(off, pairs, directions)]
    cross = [(-_dot(p[1 - d], md)).astype(_BF16) for md, p, d in zip(mid, pairs, directions)]
    zero = jnp.zeros((hs, hs), _BF16)
    out = []
    for xb, (ainv, binv), d in zip(cross, pairs, directions):
        top = jnp.concatenate([ainv, zero if d == 0 else xb], axis=1)
        bot = jnp.concatenate([xb if d == 0 else zero, binv], axis=1)
        out.append(jnp.concatenate([top, bot], axis=0))
    return out


def _dn_chunks(chains):
    nh = DN_HEADS
    dh = DN_HEAD_DIM
    lane = lax.broadcasted_iota(jnp.int32, (CHUNK, GATE_LANES), 1)
    ri = lax.broadcasted_iota(jnp.int32, (CHUNK, CHUNK), 0)
    ci = lax.broadcasted_iota(jnp.int32, (CHUNK, CHUNK), 1)
    incl = (ri >= ci, ri <= ci)
    offdiag = ri != ci

    beta, gc, l_mats, a16 = [], [], [], []
    for ch in chains:
        pick = lambda col, ch=ch: jnp.sum(jnp.where(lane == col, ch.gates, 0.0), axis=1, keepdims=True)
        b = pick(ch.direction * nh + ch.head)
        g = pick((2 + ch.direction) * nh + ch.head)
        decay = jnp.exp(jnp.where(incl[ch.direction], g - ch.gate_row, -jnp.inf))
        l_mats.append(jnp.where(offdiag, b * ch.kk * decay, 0.0))
        a16.append((ch.qk * decay).astype(_BF16))
        beta.append(b)
        gc.append(g)
    tinv = _inv_unit_tri(l_mats, [ch.direction for ch in chains])

    g_last = [g[CHUNK - 1:CHUNK, :] if ch.direction == 0 else g[0:1, :] for ch, g in zip(chains, gc)]
    kf = [ch.k.astype(_F32) for ch in chains]
    kd = [(f * jnp.exp(gl - g)).astype(_BF16) for f, gl, g in zip(kf, g_last, gc)]
    vb = [(ch.v.astype(_F32) * b).astype(_BF16) for ch, b in zip(chains, beta)]
    if chains[0].state is None:
        v_new = [_dot(t, x).astype(_BF16) for t, x in zip(tinv, vb)]
        o = [_dot(a, vn) for a, vn in zip(a16, v_new)]
        s_new = [_dot_tn(kdi, vn) for kdi, vn in zip(kd, v_new)]
        return list(zip(o, s_new))
    egc = [jnp.exp(g) for g in gc]
    rhs = [jnp.concatenate([x, (f * (b * e)).astype(_BF16)], axis=1) for x, f, b, e in zip(vb, kf, beta, egc)]
    qd = [(ch.q.astype(_F32) * e).astype(_BF16) for ch, e in zip(chains, egc)]
    s16 = [ch.state.astype(_BF16) for ch in chains]
    uw = [_dot(t, r) for t, r in zip(tinv, rhs)]
    ws = [_dot(y[:, dh:].astype(_BF16), s) for y, s in zip(uw, s16)]
    v_new = [(y[:, :dh] - w).astype(_BF16) for y, w in zip(uw, ws)]
    o = [_dot(qi, s) + _dot(a, vn) for qi, s, a, vn in zip(qd, s16, a16, v_new)]
    s_new = [ch.state * jnp.exp(gl) + _dot_tn(kdi, vn) for ch, gl, kdi, vn in zip(chains, g_last, kd, v_new)]
    return list(zip(o, s_new))


def _head_lanes(hh):
    return slice(hh * DN_HEAD_DIM, (hh + 1) * DN_HEAD_DIM)


def _dn_ctx_kernel(q_ref, k_ref, v_ref, g_ref, gt_ref, o_ref, sf_ref, sb_ref):
    hb = pl.program_id(1)
    gates = g_ref[...]
    chains = []
    for hh in range(HEADS_PER_STEP):
        head = hb * HEADS_PER_STEP + hh
        q, k, v = q_ref[:, _head_lanes(hh)], k_ref[:, _head_lanes(hh)], v_ref[:, _head_lanes(hh)]
        kk, qk = _dot_nt(k, k), _dot_nt(q, k)
        for direction in (0, 1):
            gate_row = gt_ref[pl.ds((2 + direction) * DN_HEADS + head, 1), :]
            chains.append(_Chain(direction, head, q, k, v, kk, qk, gates, gate_row, None))
    outs = _dn_chunks(chains)
    for hh in range(HEADS_PER_STEP):
        (of, sf), (ob, sb) = outs[2 * hh], outs[2 * hh + 1]
        o_ref[:, _head_lanes(hh)] = of + ob
        sf_ref[hh] = sf
        sb_ref[hh] = sb


def _dn_lat_kernel(n_chunks, q_ref, k_ref, v_ref, g_ref, gt_ref, s0f_ref, s0b_ref, o_in_ref, o_ref, s_scr):
    del o_in_ref
    hb = pl.program_id(1)
    o_ref[...] = jnp.zeros_like(o_ref)
    for hh in range(HEADS_PER_STEP):
        s_scr[2 * hh] = s0f_ref[hh]
        s_scr[2 * hh + 1] = s0b_ref[hh]

    def body(step, carry):
        chains, rows_of = [], []
        for hh in range(HEADS_PER_STEP):
            head = hb * HEADS_PER_STEP + hh
            for direction in (0, 1):
                c = step if direction == 0 else n_chunks - 1 - step
                rows = pl.ds(pl.multiple_of(c * CHUNK, CHUNK), CHUNK)
                q, k, v = q_ref[rows, _head_lanes(hh)], k_ref[rows, _head_lanes(hh)], v_ref[rows, _head_lanes(hh)]
                gate_row = gt_ref[c, pl.ds((2 + direction) * DN_HEADS + head, 1), :]
                chains.append(_Chain(direction, head, q, k, v, _dot_nt(k, k), _dot_nt(q, k),
                                     g_ref[rows, :], gate_row, s_scr[2 * hh + direction]))
                rows_of.append((rows, hh))
        for n, ((o, s_new), (rows, hh)) in enumerate(zip(_dn_chunks(chains), rows_of)):
            o_ref[rows, _head_lanes(hh)] += o
            s_scr[n] = s_new
        return carry

    lax.fori_loop(0, n_chunks, body, 0)


def _deltanet(q, k, v, gates, gates_t, s0f, s0b, n_ctx_seq, ctx_len, n_lat_seq, lat_len):
    t = q.shape[0]
    dh = DN_HEAD_DIM
    hps = HEADS_PER_STEP
    params = pltpu.CompilerParams(dimension_semantics=("arbitrary", "arbitrary"), vmem_limit_bytes=VMEM_LIMIT)
    head_spec = lambda rows, off: pl.BlockSpec((rows, hps * dh), lambda b, h: (b + off, h))
    state_spec = pl.BlockSpec((None, hps, dh, dh), lambda b, h: (b, h, 0, 0))
    state_shape = jax.ShapeDtypeStruct((n_ctx_seq, DN_HEADS, dh, dh), _F32)

    o, sf, sb = pl.pallas_call(
        _dn_ctx_kernel,
        grid=(n_ctx_seq, DN_HEADS // hps),
        in_specs=[head_spec(ctx_len, 0)] * 3 + [
            pl.BlockSpec((ctx_len, GATE_LANES), lambda b, h: (b, 0)),
            pl.BlockSpec((None, GATE_LANES, CHUNK), lambda b, h: (b, 0, 0))],
        out_specs=[head_spec(ctx_len, 0), state_spec, state_spec],
        out_shape=[jax.ShapeDtypeStruct((t, D_MODEL), _F32), state_shape, state_shape],
        compiler_params=params,
        name="deltanet_ctx",
    )(q, k, v, gates, gates_t)

    off = (n_ctx_seq * ctx_len) // lat_len
    o = pl.pallas_call(
        functools.partial(_dn_lat_kernel, lat_len // CHUNK),
        grid=(n_lat_seq, DN_HEADS // hps),
        in_specs=[head_spec(lat_len, off)] * 3 + [
            pl.BlockSpec((lat_len, GATE_LANES), lambda b, h: (b + off, 0)),
            pl.BlockSpec((lat_len // CHUNK, GATE_LANES, CHUNK), lambda b, h: (b + off, 0, 0)),
            state_spec, state_spec,
            pl.BlockSpec(memory_space=pl.ANY)],
        out_specs=head_spec(lat_len, off),
        out_shape=jax.ShapeDtypeStruct((t, D_MODEL), _F32),
        input_output_aliases={7: 0},
        scratch_shapes=[pltpu.VMEM((2 * hps, dh, dh), _F32)],
        compiler_params=params,
        name="deltanet_lat",
    )(q, k, v, gates, gates_t, s0f, s0b, o)
    return o, sf, sb


def _mix_out_kernel(o_ref, sz_ref, sgb_ref, ma_ref, x_ref, mod_ref, dnw_ref, dow_ref, wo_ref,
                    nw_ref, wg_ref, wu_ref, wd_ref, nf_ref, y_ref):
    dh = DN_HEAD_DIM
    o = o_ref[...]
    heads = []
    for h in range(DN_HEADS):
        oh = o[:, h * dh:(h + 1) * dh]
        heads.append(oh * lax.rsqrt(jnp.mean(oh * oh, axis=-1, keepdims=True) + EPS) * dnw_ref[...])
    on = (jnp.concatenate(heads, axis=1) * sz_ref[...].astype(_F32)).astype(_BF16)
    yb = _dot(on, dow_ref[...])
    merged = (ma_ref[...].astype(_F32) + sgb_ref[...].astype(_F32) * yb).astype(_BF16)
    x = x_ref[...] + mod_ref[5:6, :] * _dot(merged, wo_ref[...])
    x = _ffn(x, nw_ref[...], mod_ref[6:7, :], mod_ref[7:8, :], mod_ref[8:9, :], wg_ref, wu_ref, wd_ref)
    y_ref[...] = _rms(x) * nf_ref[...]


def _mix_out(o, sz, sgb, ma, x, mod, group, dn_norm_w, dn_out_w, w_o, nw, wg, wu, wd, norm_f):
    t = x.shape[0]
    return pl.pallas_call(
        _mix_out_kernel,
        grid=(t // ROW_TILE,),
        in_specs=[_row_spec(D_MODEL)] * 5 + [_mod_spec(group), _resident((1, DN_HEAD_DIM)),
                  _resident(dn_out_w.shape), _resident(w_o.shape), _resident((1, D_MODEL)),
                  _resident(wg.shape), _resident(wu.shape), _resident(wd.shape), _resident((1, D_MODEL))],
        out_specs=_row_spec(D_MODEL),
        out_shape=jax.ShapeDtypeStruct((t, D_MODEL), _F32),
        compiler_params=_params(),
        name="mix_out",
    )(o, sz, sgb, ma, x, mod, dn_norm_w, dn_out_w, w_o, nw, wg, wu, wd, norm_f)


def _grid_pos_embed(n_rows):
    t = jnp.arange(n_rows * GRID_W)
    r = (t // GRID_W).astype(_F32)
    col = (t % GRID_W).astype(_F32)
    quarter = D_MODEL // 4
    omega = 1.0 / (POS_BASE ** (jnp.arange(quarter, dtype=_F32) / quarter))
    ar = r[:, None] * omega
    ac = col[:, None] * omega
    return jnp.concatenate([jnp.sin(ar), jnp.cos(ar), jnp.sin(ac), jnp.cos(ac)], axis=-1)


def kernel(x_prompt, x_sample, state_dn_fwd, state_dn_bwd, c, c_ctx, ada_w, ada_b, norm_ffn1, ffn1_w_gate, ffn1_w_up, ffn1_w_down, norm_mix, w_in, conv_w, conv_out_w, dn_conv_w, dn_a_log, dn_dt_bias, dn_norm_w, dn_out_w, w_o, norm_ffn2, ffn2_w_gate, ffn2_w_up, ffn2_w_down, norm_f):
    n_ctx_seq, ctx_len, d = x_prompt.shape
    n_lat_seq, lat_len, _ = x_sample.shape
    depth = ada_w.shape[0]
    assert depth == 1 and d == D_MODEL
    assert ctx_len == CHUNK and lat_len % CHUNK == 0 and lat_len % ROW_TILE == 0
    assert (n_ctx_seq * ctx_len) % lat_len == 0 and CHUNK % GRID_W == 0 and n_lat_seq == 2
    n_ctx_tiles = (n_ctx_seq * ctx_len) // ROW_TILE
    group = _group_of_tile(n_ctx_tiles, lat_len // ROW_TILE)
    bf = lambda w: w.astype(_BF16)
    row = lambda w: w.reshape(1, -1)

    pos = _grid_pos_embed(lat_len // GRID_W).astype(x_sample.dtype)
    x = jnp.concatenate([x_prompt.reshape(-1, d), (x_sample + pos[None]).reshape(-1, d)], axis=0)

    cond_t = jnp.zeros((d, 8), _F32).at[:, 0].set(c_ctx).at[:, 1:1 + n_lat_seq].set(c.T)
    mod = _ada(cond_t, ada_w[0], row(ada_b[0])).reshape(8, N_ADA, d)

    x = _ffn1(x, mod, group, row(norm_ffn1[0]), bf(ffn1_w_gate[0]), bf(ffn1_w_up[0]), bf(ffn1_w_down[0]))

    n_main = 9 * d
    n_gate = 4 * DN_HEADS
    w_ba = jnp.zeros((d, GATE_LANES), _F32).at[:, :n_gate].set(w_in[0][:, n_main:n_main + n_gate])
    gate_row = lambda p: jnp.zeros((1, GATE_LANES), _F32).at[0, 2 * DN_HEADS:n_gate].set(p.reshape(-1))
    ma, q, k, v, sz, sgb, gates, gates_t = _mix_in(
        x, mod, group, n_ctx_tiles, row(norm_mix[0]), bf(w_in[0][:, :n_main]), w_ba, conv_w[0],
        bf(conv_out_w[0]), dn_conv_w[0], gate_row(dn_a_log[0]), gate_row(dn_dt_bias[0]))

    o, sf, sb = _deltanet(q, k, v, gates, gates_t, state_dn_fwd[:, 0], state_dn_bwd[:, 0],
                          n_ctx_seq, ctx_len, n_lat_seq, lat_len)

    y = _mix_out(o, sz, sgb, ma, x, mod, group, row(dn_norm_w[0]), bf(dn_out_w[0]), bf(w_o[0]),
                 row(norm_ffn2[0]), bf(ffn2_w_gate[0]), bf(ffn2_w_up[0]), bf(ffn2_w_down[0]), row(norm_f))

    n_ctx = n_ctx_seq * ctx_len
    y_prompt = y[:n_ctx].reshape(x_prompt.shape)
    y_sample = y[n_ctx:].reshape(x_sample.shape)
    return (y_prompt, y_sample, sf[:, None].astype(x_prompt.dtype), sb[:, None].astype(x_prompt.dtype))
```

```python
import functools
from typing import Any, NamedTuple

import jax
import jax.numpy as jnp
from jax import lax
from jax.experimental import pallas as pl
from jax.experimental.pallas import tpu as pltpu

D_MODEL = 1024
D_FF = 2816
DN_HEADS = 8
DN_HEAD_DIM = 128
N_ADA = 9
GRID_W = 64
EPS = 1e-6
POS_BASE = 10000.0

CHUNK = 256
HALF = CHUNK // 2
HEADS_PER_STEP = 4
ROW_TILE = 512
FF_CHUNK = 256
GATE_LANES = 128
VMEM_LIMIT = 56 * 1024 * 1024

_F32 = jnp.float32
_BF16 = jnp.bfloat16


def _dot(a, b):
    return jnp.dot(a, b, preferred_element_type=_F32)


def _dot_nt(a, b):
    return lax.dot_general(a, b, (((1,), (1,)), ((), ())), preferred_element_type=_F32)


def _dot_tn(a, b):
    return lax.dot_general(a, b, (((0,), (0,)), ((), ())), preferred_element_type=_F32)


def _sigmoid(x):
    return 1.0 / (1.0 + jnp.exp(-x))


def _silu(x):
    return x * _sigmoid(x)


def _rms(x):
    return x * lax.rsqrt(jnp.mean(x * x, axis=-1, keepdims=True) + EPS)


def _resident(shape):
    nd = len(shape)
    return pl.BlockSpec(shape, lambda *_: (0,) * nd, pipeline_mode=pl.Buffered(1))


def _params():
    return pltpu.CompilerParams(dimension_semantics=("arbitrary",), vmem_limit_bytes=VMEM_LIMIT)


def _ada_kernel(ct_ref, w_ref, b_ref, o_ref):
    ct = ct_ref[...]
    s = _silu(ct)
    w = w_ref[...]
    o_ref[...] = jnp.zeros_like(o_ref)
    for r in range(3):
        o_ref[r:r + 1, :] = jnp.sum(s[:, r:r + 1] * w, axis=0, keepdims=True) + b_ref[...]


def _ada(cond_t, ada_w, ada_b):
    n = ada_w.shape[1]
    tn = 1024
    return pl.pallas_call(
        _ada_kernel,
        grid=(n // tn,),
        in_specs=[pl.BlockSpec((D_MODEL, 8), lambda j: (0, 0)),
                  pl.BlockSpec((D_MODEL, tn), lambda j: (0, j)),
                  pl.BlockSpec((1, tn), lambda j: (0, j))],
        out_specs=pl.BlockSpec((8, tn), lambda j: (0, j)),
        out_shape=jax.ShapeDtypeStruct((8, n), _F32),
        compiler_params=_params(),
        name="ada",
    )(cond_t, ada_w, ada_b)


def _ffn(x, nw, sh, sc, gate, wg_ref, wu_ref, wd_ref):
    u = (_rms(x) * nw * (1.0 + sc) + sh).astype(_BF16)
    acc = None
    for c in range(D_FF // FF_CHUNK):
        sl = slice(c * FF_CHUNK, (c + 1) * FF_CHUNK)
        a = _dot(u, wg_ref[:, sl])
        b = _dot(u, wu_ref[:, sl])
        h = (_silu(a) * b).astype(_BF16)
        d = _dot(h, wd_ref[sl, :])
        acc = d if acc is None else acc + d
    return x + (0.5 * gate) * acc


def _pos_embed_tile(omega, tile_in_seq):
    grid_rows = ROW_TILE // GRID_W
    quarter = omega.shape[1]
    r = (tile_in_seq * grid_rows + lax.broadcasted_iota(jnp.int32, (grid_rows, 1), 0)).astype(_F32)
    col = lax.broadcasted_iota(jnp.int32, (GRID_W, 1), 0).astype(_F32)
    ar, ac = r * omega, col * omega
    sin_r, cos_r, sin_c, cos_c = jnp.sin(ar), jnp.cos(ar), jnp.sin(ac), jnp.cos(ac)
    rows = []
    for n in range(grid_rows):
        up = lambda a: jnp.broadcast_to(a[n:n + 1, :], (GRID_W, quarter))
        rows.append(jnp.concatenate([up(sin_r), up(cos_r), sin_c, cos_c], axis=1))
    return jnp.concatenate(rows, axis=0)


def _ffn1_kernel(n_ctx_tiles, tiles_per_latent, xp_ref, xs_ref, omega_ref, mod_ref, nw_ref,
                 wg_ref, wu_ref, wd_ref, o_ref):
    i = pl.program_id(0)
    tile_in_seq = jnp.maximum(i - n_ctx_tiles, 0) % tiles_per_latent
    latent = xs_ref[...] + _pos_embed_tile(omega_ref[...], tile_in_seq)
    x = jnp.where(i < n_ctx_tiles, xp_ref[...], latent)
    o_ref[...] = _ffn(x, nw_ref[...], mod_ref[0:1, :], mod_ref[1:2, :], mod_ref[2:3, :],
                      wg_ref, wu_ref, wd_ref)


def _group_of_tile(n_ctx_tiles, tiles_per_latent):
    def f(i):
        return jnp.where(i < n_ctx_tiles, 0, 1 + (i - n_ctx_tiles) // tiles_per_latent)
    return f


def _row_spec(width):
    return pl.BlockSpec((ROW_TILE, width), lambda i: (i, 0))


def _mod_spec(group):
    return pl.BlockSpec((None, N_ADA, D_MODEL), lambda i: (group(i), 0, 0))


def _ctx_spec(n_ctx_tiles, width=D_MODEL):
    return pl.BlockSpec((ROW_TILE, width), lambda i: (jnp.minimum(i, n_ctx_tiles - 1), 0))


def _lat_spec(n_ctx_tiles, width=D_MODEL):
    return pl.BlockSpec((ROW_TILE, width), lambda i: (jnp.maximum(i - n_ctx_tiles, 0), 0))


def _ffn1(xp, xs, omega, mod, group, n_ctx_tiles, tiles_per_latent, nw, wg, wu, wd):
    t = xp.shape[0] + xs.shape[0]
    return pl.pallas_call(
        functools.partial(_ffn1_kernel, n_ctx_tiles, tiles_per_latent),
        grid=(t // ROW_TILE,),
        in_specs=[_ctx_spec(n_ctx_tiles), _lat_spec(n_ctx_tiles), _resident(omega.shape),
                  _mod_spec(group), _resident((1, D_MODEL)),
                  _resident(wg.shape), _resident(wu.shape), _resident(wd.shape)],
        out_specs=_row_spec(D_MODEL),
        out_shape=jax.ShapeDtypeStruct((t, D_MODEL), _F32),
        compiler_params=_params(),
        name="ffn1",
    )(xp, xs, omega, mod, nw, wg, wu, wd)


def _conv3(x, w_ref, lanes, pos, last):
    m = x.shape[0]
    prev = jnp.where(pos == 0, 0.0, pltpu.roll(x, 1, axis=0))
    nxt = jnp.where(pos == last, 0.0, pltpu.roll(x, m - 1, axis=0))
    return prev * w_ref[0:1, lanes] + x * w_ref[1:2, lanes] + nxt * w_ref[2:3, lanes]


def _split3(x):
    hi = x.astype(_BF16)
    r = x - hi.astype(_F32)
    mid = r.astype(_BF16)
    lo = (r - mid.astype(_F32)).astype(_BF16)
    return hi, mid, lo


def _mix_in_kernel(n_ctx_tiles, x_ref, mod_ref, nw_ref, win_ref, wba_ref, cw_ref, cow_ref, dcw_ref,
                   alog_ref, dtb_ref, ma_ref, q_ref, k_ref, v_ref, sz_ref, sgb_ref, g_ref, gt_ref):
    i = pl.program_id(0)
    d = D_MODEL
    uf = _rms(x_ref[...]) * nw_ref[...] * (1.0 + mod_ref[4:5, :]) + mod_ref[3:4, :]
    u = uf.astype(_BF16)

    row = lax.broadcasted_iota(jnp.int32, (ROW_TILE, 1), 0)
    last = jnp.where(i < n_ctx_tiles, CHUNK - 1, GRID_W - 1)
    pos = row & last
    full = slice(0, d)

    bg = _dot(u, win_ref[:, 0:d])
    cg = _dot(u, win_ref[:, d:2 * d])
    xa = _dot(u, win_ref[:, 2 * d:3 * d])
    a_pre = (bg * _conv3(cg * xa, cw_ref, full, pos, last)).astype(_BF16)
    ya = _dot(a_pre, cow_ref[...])
    ga = _dot(u, win_ref[:, 7 * d:8 * d])
    ma_ref[...] = (_sigmoid(ga) * ya).astype(_BF16)

    for part, ref in enumerate((q_ref, k_ref, v_ref)):
        p = _dot(u, win_ref[:, (3 + part) * d:(4 + part) * d])
        p = _silu(_conv3(p, dcw_ref, slice(part * d, (part + 1) * d), pos, last))
        if part == 2:
            ref[...] = p.astype(_BF16)
        else:
            scale = DN_HEAD_DIM ** -0.5 if part == 0 else 1.0
            for h in range(DN_HEADS):
                ph = p[:, h * DN_HEAD_DIM:(h + 1) * DN_HEAD_DIM]
                ph = ph * lax.rsqrt(jnp.sum(ph * ph, axis=-1, keepdims=True) + EPS)
                ref[:, h * DN_HEAD_DIM:(h + 1) * DN_HEAD_DIM] = (ph * scale).astype(_BF16)

    sz_ref[...] = _silu(_dot(u, win_ref[:, 6 * d:7 * d])).astype(_BF16)
    sgb_ref[...] = _sigmoid(_dot(u, win_ref[:, 8 * d:9 * d])).astype(_BF16)

    u_lo = (uf - u.astype(_F32)).astype(_BF16)
    wba = wba_ref[...]
    w_hi = wba.astype(_BF16)
    w_lo = (wba - w_hi.astype(_F32)).astype(_BF16)
    ba = _dot(u, w_hi) + _dot(u, w_lo) + _dot(u_lo, w_hi)
    beta = _sigmoid(ba)
    z = ba + dtb_ref[...]
    softplus = jnp.maximum(z, 0.0) + jnp.log(1.0 + jnp.exp(-jnp.abs(z)))
    g = -jnp.exp(alog_ref[...]) * softplus

    lane = lax.broadcasted_iota(jnp.int32, (CHUNK, GATE_LANES), 1)
    ri = lax.broadcasted_iota(jnp.int32, (CHUNK, CHUNK), 0)
    ci = lax.broadcasted_iota(jnp.int32, (CHUNK, CHUNK), 1)
    lower = jnp.where(ri >= ci, 1.0, 0.0).astype(_BF16)
    upper = jnp.where(ri <= ci, 1.0, 0.0).astype(_BF16)
    nh = DN_HEADS
    for c in range(ROW_TILE // CHUNK):
        rows = slice(c * CHUNK, (c + 1) * CHUNK)
        parts = _split3(g[rows, :])
        pre = sum(_dot(lower, t) for t in parts)
        suf = sum(_dot(upper, t) for t in parts)
        tab = jnp.where(lane < 2 * nh, beta[rows, :],
                        jnp.where(lane < 3 * nh, pre, jnp.where(lane < 4 * nh, suf, 0.0)))
        g_ref[rows, :] = tab
        gt_ref[c] = tab.T


def _mix_in(x, mod, group, n_ctx_tiles, nw, w_in, w_ba, conv_w, conv_out_w, dn_conv_w, alog_row, dtb_row):
    t = x.shape[0]
    bf = lambda w: jax.ShapeDtypeStruct((t, w), _BF16)
    return pl.pallas_call(
        functools.partial(_mix_in_kernel, n_ctx_tiles),
        grid=(t // ROW_TILE,),
        in_specs=[_row_spec(D_MODEL), _mod_spec(group), _resident((1, D_MODEL)),
                  _resident(w_in.shape), _resident(w_ba.shape), _resident(conv_w.shape),
                  _resident(conv_out_w.shape), _resident(dn_conv_w.shape),
                  _resident((1, GATE_LANES)), _resident((1, GATE_LANES))],
        out_specs=[_row_spec(D_MODEL)] * 6 + [
            _row_spec(GATE_LANES),
            pl.BlockSpec((ROW_TILE // CHUNK, GATE_LANES, CHUNK), lambda i: (i, 0, 0))],
        out_shape=[bf(D_MODEL)] * 6 + [jax.ShapeDtypeStruct((t, GATE_LANES), _F32),
                                       jax.ShapeDtypeStruct((t // CHUNK, GATE_LANES, CHUNK), _F32)],
        compiler_params=_params(),
        name="mix_in",
    )(x, mod, nw, w_in, w_ba, conv_w, conv_out_w, dn_conv_w, alog_row, dtb_row)


class _Chain(NamedTuple):
    direction: int
    head: Any
    q: Any
    k: Any
    v: Any
    kk: Any
    qk: Any
    gates: Any
    gate_row: Any
    state: Any


def _inv_unit_tri(l_mats, directions):
    hs = HALF
    ri = lax.broadcasted_iota(jnp.int32, (hs, hs), 0)
    ci = lax.broadcasted_iota(jnp.int32, (hs, hs), 1)
    x = ri ^ ci
    diag = [d for l in l_mats for d in (l[:hs, :hs], l[hs:, hs:])]
    eye = jnp.where(x == 0, 1.0, 0.0)
    tinv = [eye - jnp.where(x == 1, d, 0.0) for d in diag]
    m = 2
    while m < hs:
        level = (x >= m) & (x < 2 * m)
        c16 = [jnp.where(level, d, 0.0).astype(_BF16) for d in diag]
        t16 = [t.astype(_BF16) for t in tinv]
        mid = [_dot(c, t).astype(_BF16) for c, t in zip(c16, t16)]
        tinv = [t - _dot(t6, md) for t, t6, md in zip(tinv, t16, mid)]
        m *= 2
    t16 = [t.astype(_BF16) for t in tinv]
    pairs = [(t16[2 * n], t16[2 * n + 1]) for n in range(len(l_mats))]
    off = [(l[hs:, :hs] if d == 0 else l[:hs, hs:]).astype(_BF16) for l, d in zip(l_mats, directions)]
    mid = [_dot(c, p[d]).astype(_BF16) for c, p, d in zip(off, pairs, directions)]
    cross = [(-_dot(p[1 - d], md)).astype(_BF16) for md, p, d in zip(mid, pairs, directions)]
    zero = jnp.zeros((hs, hs), _BF16)
    out = []
    for xb, (ainv, binv), d in zip(cross, pairs, directions):
        top = jnp.concatenate([ainv, zero if d == 0 else xb], axis=1)
        bot = jnp.concatenate([xb if d == 0 else zero, binv], axis=1)
        out.append(jnp.concatenate([top, bot], axis=0))
    return out


def _dn_chunks(chains):
    nh = DN_HEADS
    dh = DN_HEAD_DIM
    lane = lax.broadcasted_iota(jnp.int32, (CHUNK, GATE_LANES), 1)
    ri = lax.broadcasted_iota(jnp.int32, (CHUNK, CHUNK), 0)
    ci = lax.broadcasted_iota(jnp.int32, (CHUNK, CHUNK), 1)
    incl = (ri >= ci, ri <= ci)
    offdiag = ri != ci

    beta, gc, l_mats, a16 = [], [], [], []
    for ch in chains:
        pick = lambda col, ch=ch: jnp.sum(jnp.where(lane == col, ch.gates, 0.0), axis=1, keepdims=True)
        b = pick(ch.direction * nh + ch.head)
        g = pick((2 + ch.direction) * nh + ch.head)
        decay = jnp.exp(jnp.where(incl[ch.direction], g - ch.gate_row, -jnp.inf))
        l_mats.append(jnp.where(offdiag, b * ch.kk * decay, 0.0))
        a16.append((ch.qk * decay).astype(_BF16))
        beta.append(b)
        gc.append(g)
    tinv = _inv_unit_tri(l_mats, [ch.direction for ch in chains])

    g_last = [g[CHUNK - 1:CHUNK, :] if ch.direction == 0 else g[0:1, :] for ch, g in zip(chains, gc)]
    kf = [ch.k.astype(_F32) for ch in chains]
    kd = [(f * jnp.exp(gl - g)).astype(_BF16) for f, gl, g in zip(kf, g_last, gc)]
    vb = [(ch.v.astype(_F32) * b).astype(_BF16) for ch, b in zip(chains, beta)]
    if chains[0].state is None:
        v_new = [_dot(t, x).astype(_BF16) for t, x in zip(tinv, vb)]
        o = [_dot(a, vn) for a, vn in zip(a16, v_new)]
        s_new = [_dot_tn(kdi, vn) for kdi, vn in zip(kd, v_new)]
        return list(zip(o, s_new))
    egc = [jnp.exp(g) for g in gc]
    rhs = [jnp.concatenate([x, (f * (b * e)).astype(_BF16)], axis=1) for x, f, b, e in zip(vb, kf, beta, egc)]
    qd = [(ch.q.astype(_F32) * e).astype(_BF16) for ch, e in zip(chains, egc)]
    s16 = [ch.state.astype(_BF16) for ch in chains]
    uw = [_dot(t, r) for t, r in zip(tinv, rhs)]
    ws = [_dot(y[:, dh:].astype(_BF16), s) for y, s in zip(uw, s16)]
    v_new = [(y[:, :dh] - w).astype(_BF16) for y, w in zip(uw, ws)]
    o = [_dot(qi, s) + _dot(a, vn) for qi, s, a, vn in zip(qd, s16, a16, v_new)]
    s_new = [ch.state * jnp.exp(gl) + _dot_tn(kdi, vn) for ch, gl, kdi, vn in zip(chains, g_last, kd, v_new)]
    return list(zip(o, s_new))


def _head_lanes(hh):
    return slice(hh * DN_HEAD_DIM, (hh + 1) * DN_HEAD_DIM)


def _dn_ctx_kernel(q_ref, k_ref, v_ref, g_ref, gt_ref, o_ref, sf_ref, sb_ref):
    hb = pl.program_id(1)
    gates = g_ref[...]
    chains = []
    for hh in range(HEADS_PER_STEP):
        head = hb * HEADS_PER_STEP + hh
        q, k, v = q_ref[:, _head_lanes(hh)], k_ref[:, _head_lanes(hh)], v_ref[:, _head_lanes(hh)]
        kk, qk = _dot_nt(k, k), _dot_nt(q, k)
        for direction in (0, 1):
            gate_row = gt_ref[pl.ds((2 + direction) * DN_HEADS + head, 1), :]
            chains.append(_Chain(direction, head, q, k, v, kk, qk, gates, gate_row, None))
    outs = _dn_chunks(chains)
    for hh in range(HEADS_PER_STEP):
        (of, sf), (ob, sb) = outs[2 * hh], outs[2 * hh + 1]
        o_ref[:, _head_lanes(hh)] = of + ob
        sf_ref[hh] = sf
        sb_ref[hh] = sb


def _dn_lat_kernel(n_chunks, q_ref, k_ref, v_ref, g_ref, gt_ref, s0f_ref, s0b_ref, o_ref, s_scr):
    hb = pl.program_id(1)
    o_ref[...] = jnp.zeros_like(o_ref)
    for hh in range(HEADS_PER_STEP):
        s_scr[2 * hh] = s0f_ref[hh]
        s_scr[2 * hh + 1] = s0b_ref[hh]

    def body(step, carry):
        chains, rows_of = [], []
        for hh in range(HEADS_PER_STEP):
            head = hb * HEADS_PER_STEP + hh
            for direction in (0, 1):
                c = step if direction == 0 else n_chunks - 1 - step
                rows = pl.ds(pl.multiple_of(c * CHUNK, CHUNK), CHUNK)
                q, k, v = q_ref[rows, _head_lanes(hh)], k_ref[rows, _head_lanes(hh)], v_ref[rows, _head_lanes(hh)]
                gate_row = gt_ref[c, pl.ds((2 + direction) * DN_HEADS + head, 1), :]
                chains.append(_Chain(direction, head, q, k, v, _dot_nt(k, k), _dot_nt(q, k),
                                     g_ref[rows, :], gate_row, s_scr[2 * hh + direction]))
                rows_of.append((rows, hh))
        for n, ((o, s_new), (rows, hh)) in enumerate(zip(_dn_chunks(chains), rows_of)):
            o_ref[rows, _head_lanes(hh)] += o
            s_scr[n] = s_new
        return carry

    lax.fori_loop(0, n_chunks, body, 0)


def _deltanet(q, k, v, gates, gates_t, s0f, s0b, n_ctx_seq, ctx_len, n_lat_seq, lat_len):
    t = q.shape[0]
    dh = DN_HEAD_DIM
    hps = HEADS_PER_STEP
    params = pltpu.CompilerParams(dimension_semantics=("arbitrary", "arbitrary"), vmem_limit_bytes=VMEM_LIMIT)
    head_spec = lambda rows, off: pl.BlockSpec((rows, hps * dh), lambda b, h: (b + off, h))
    state_spec = pl.BlockSpec((None, hps, dh, dh), lambda b, h: (b, h, 0, 0))
    state_shape = jax.ShapeDtypeStruct((n_ctx_seq, DN_HEADS, dh, dh), _F32)
    n_ctx = n_ctx_seq * ctx_len

    o_ctx, sf, sb = pl.pallas_call(
        _dn_ctx_kernel,
        grid=(n_ctx_seq, DN_HEADS // hps),
        in_specs=[head_spec(ctx_len, 0)] * 3 + [
            pl.BlockSpec((ctx_len, GATE_LANES), lambda b, h: (b, 0)),
            pl.BlockSpec((None, GATE_LANES, CHUNK), lambda b, h: (b, 0, 0))],
        out_specs=[head_spec(ctx_len, 0), state_spec, state_spec],
        out_shape=[jax.ShapeDtypeStruct((n_ctx, D_MODEL), _F32), state_shape, state_shape],
        compiler_params=params,
        name="deltanet_ctx",
    )(q, k, v, gates, gates_t)

    off = n_ctx // lat_len
    o_lat = pl.pallas_call(
        functools.partial(_dn_lat_kernel, lat_len // CHUNK),
        grid=(n_lat_seq, DN_HEADS // hps),
        in_specs=[head_spec(lat_len, off)] * 3 + [
            pl.BlockSpec((lat_len, GATE_LANES), lambda b, h: (b + off, 0)),
            pl.BlockSpec((lat_len // CHUNK, GATE_LANES, CHUNK), lambda b, h: (b + off, 0, 0)),
            state_spec, state_spec],
        out_specs=head_spec(lat_len, 0),
        out_shape=jax.ShapeDtypeStruct((t - n_ctx, D_MODEL), _F32),
        scratch_shapes=[pltpu.VMEM((2 * hps, dh, dh), _F32)],
        compiler_params=params,
        name="deltanet_lat",
    )(q, k, v, gates, gates_t, s0f, s0b)
    return o_ctx, o_lat, sf, sb


def _mix_out_kernel(n_ctx_tiles, oc_ref, ol_ref, sz_ref, sgb_ref, ma_ref, x_ref, mod_ref, dnw_ref, dow_ref,
                    wo_ref, nw_ref, wg_ref, wu_ref, wd_ref, nf_ref, yp_ref, ys_ref):
    dh = DN_HEAD_DIM
    i = pl.program_id(0)
    o = jnp.where(i < n_ctx_tiles, oc_ref[...], ol_ref[...])
    heads = []
    for h in range(DN_HEADS):
        oh = o[:, h * dh:(h + 1) * dh]
        heads.append(oh * lax.rsqrt(jnp.mean(oh * oh, axis=-1, keepdims=True) + EPS) * dnw_ref[...])
    on = (jnp.concatenate(heads, axis=1) * sz_ref[...].astype(_F32)).astype(_BF16)
    yb = _dot(on, dow_ref[...])
    merged = (ma_ref[...].astype(_F32) + sgb_ref[...].astype(_F32) * yb).astype(_BF16)
    x = x_ref[...] + mod_ref[5:6, :] * _dot(merged, wo_ref[...])
    x = _ffn(x, nw_ref[...], mod_ref[6:7, :], mod_ref[7:8, :], mod_ref[8:9, :], wg_ref, wu_ref, wd_ref)
    y = _rms(x) * nf_ref[...]

    @pl.when(i < n_ctx_tiles)
    def _():
        yp_ref[...] = y

    @pl.when(i >= n_ctx_tiles)
    def _():
        ys_ref[...] = y


def _mix_out(o_ctx, o_lat, sz, sgb, ma, x, mod, group, n_ctx_tiles, dn_norm_w, dn_out_w, w_o, nw, wg, wu, wd,
             norm_f):
    t = x.shape[0]
    n_ctx = n_ctx_tiles * ROW_TILE
    return pl.pallas_call(
        functools.partial(_mix_out_kernel, n_ctx_tiles),
        grid=(t // ROW_TILE,),
        in_specs=[_ctx_spec(n_ctx_tiles), _lat_spec(n_ctx_tiles)] + [_row_spec(D_MODEL)] * 4 + [
            _mod_spec(group), _resident((1, DN_HEAD_DIM)),
            _resident(dn_out_w.shape), _resident(w_o.shape), _resident((1, D_MODEL)),
            _resident(wg.shape), _resident(wu.shape), _resident(wd.shape), _resident((1, D_MODEL))],
        out_specs=[_ctx_spec(n_ctx_tiles), _lat_spec(n_ctx_tiles)],
        out_shape=[jax.ShapeDtypeStruct((n_ctx, D_MODEL), _F32), jax.ShapeDtypeStruct((t - n_ctx, D_MODEL), _F32)],
        compiler_params=_params(),
        name="mix_out",
    )(o_ctx, o_lat, sz, sgb, ma, x, mod, dn_norm_w, dn_out_w, w_o, nw, wg, wu, wd, norm_f)


def kernel(x_prompt, x_sample, state_dn_fwd, state_dn_bwd, c, c_ctx, ada_w, ada_b, norm_ffn1, ffn1_w_gate, ffn1_w_up, ffn1_w_down, norm_mix, w_in, conv_w, conv_out_w, dn_conv_w, dn_a_log, dn_dt_bias, dn_norm_w, dn_out_w, w_o, norm_ffn2, ffn2_w_gate, ffn2_w_up, ffn2_w_down, norm_f):
    n_ctx_seq, ctx_len, d = x_prompt.shape
    n_lat_seq, lat_len, _ = x_sample.shape
    depth = ada_w.shape[0]
    assert depth == 1 and d == D_MODEL
    assert ctx_len == CHUNK and lat_len % CHUNK == 0 and lat_len % ROW_TILE == 0
    assert (n_ctx_seq * ctx_len) % lat_len == 0 and CHUNK % GRID_W == 0 and n_lat_seq == 2
    n_ctx_tiles = (n_ctx_seq * ctx_len) // ROW_TILE
    group = _group_of_tile(n_ctx_tiles, lat_len // ROW_TILE)
    bf = lambda w: w.astype(_BF16)
    row = lambda w: w.reshape(1, -1)

    cond_t = jnp.zeros((d, 8), _F32).at[:, 0].set(c_ctx).at[:, 1:1 + n_lat_seq].set(c.T)
    mod = _ada(cond_t, ada_w[0], row(ada_b[0])).reshape(8, N_ADA, d)

    quarter = d // 4
    omega = row(1.0 / (POS_BASE ** (jnp.arange(quarter, dtype=_F32) / quarter)))
    x = _ffn1(x_prompt.reshape(-1, d), x_sample.reshape(-1, d), omega, mod, group, n_ctx_tiles,
              lat_len // ROW_TILE, row(norm_ffn1[0]), bf(ffn1_w_gate[0]), bf(ffn1_w_up[0]), bf(ffn1_w_down[0]))

    n_main = 9 * d
    n_gate = 4 * DN_HEADS
    w_ba = jnp.zeros((d, GATE_LANES), _F32).at[:, :n_gate].set(w_in[0][:, n_main:n_main + n_gate])
    gate_row = lambda p: jnp.zeros((1, GATE_LANES), _F32).at[0, 2 * DN_HEADS:n_gate].set(p.reshape(-1))
    ma, q, k, v, sz, sgb, gates, gates_t = _mix_in(
        x, mod, group, n_ctx_tiles, row(norm_mix[0]), bf(w_in[0][:, :n_main]), w_ba, conv_w[0],
        bf(conv_out_w[0]), dn_conv_w[0], gate_row(dn_a_log[0]), gate_row(dn_dt_bias[0]))

    o_ctx, o_lat, sf, sb = _deltanet(q, k, v, gates, gates_t, state_dn_fwd[:, 0], state_dn_bwd[:, 0],
                                     n_ctx_seq, ctx_len, n_lat_seq, lat_len)

    y_prompt, y_sample = _mix_out(
        o_ctx, o_lat, sz, sgb, ma, x, mod, group, n_ctx_tiles, row(dn_norm_w[0]), bf(dn_out_w[0]), bf(w_o[0]),
        row(norm_ffn2[0]), bf(ffn2_w_gate[0]), bf(ffn2_w_up[0]), bf(ffn2_w_down[0]), row(norm_f))

    return (y_prompt.reshape(x_prompt.shape), y_sample.reshape(x_sample.shape),
            sf[:, None].astype(x_prompt.dtype), sb[:, None].astype(x_prompt.dtype))
```
